```python
import jax, jax.numpy as jnp
from jax import lax
import numpy as np

D_MODEL = 1024
BATCH = 8
SEQ = 2048
DEPTH = 1

N_RET_HEADS = 8
RET_HEAD_DIM = D_MODEL // N_RET_HEADS
RET_WIDTH = N_RET_HEADS * RET_HEAD_DIM
RET_CHUNK = 128
ROPE_BASE = 10000.0
CONV_WIDTH = D_MODEL
CONV_GROUPS = 8
CONV_K = 3
FFN_HIDDEN = -(-8 * D_MODEL // (3 * 256)) * 256
EPS = 1e-6
N_MOD = 6
IN_SPLITS = [RET_WIDTH, RET_WIDTH, RET_WIDTH, RET_WIDTH,
             CONV_WIDTH, CONV_WIDTH, CONV_WIDTH,
             D_MODEL, D_MODEL]
IN_COLS = sum(IN_SPLITS)

kernel_name = "hybrid_retention_shortconv_block"


def rmsnorm(x, g):
    xf = x.astype(jnp.float32)
    y = xf * lax.rsqrt(jnp.mean(xf * xf, axis=-1, keepdims=True) + EPS)
    return (y * g.astype(jnp.float32)).astype(x.dtype)


def head_layernorm(y):
    yf = y.astype(jnp.float32)
    mu = jnp.mean(yf, axis=-1, keepdims=True)
    var = jnp.mean((yf - mu) ** 2, axis=-1, keepdims=True)
    return ((yf - mu) * lax.rsqrt(var + EPS)).astype(y.dtype)


def modulate(h, shift, scale):
    return h * (1.0 + scale[:, None, :]) + shift[:, None, :]


def rope(t, cos, sin):
    t1, t2 = jnp.split(t, 2, axis=-1)
    out = jnp.concatenate([t1 * cos - t2 * sin, t1 * sin + t2 * cos], axis=-1)
    return out.astype(t.dtype)


def retention_chunkwise(q, k, v, log_gamma):
    B, S, H, d = q.shape
    N = S // RET_CHUNK

    def to_chunks(t):
        return t.reshape(B, N, RET_CHUNK, H, t.shape[-1]).transpose(0, 3, 1, 2, 4)

    qc, kc, vc = to_chunks(q), to_chunks(k), to_chunks(v)
    idx = jnp.arange(RET_CHUNK, dtype=jnp.float32)
    rel = idx[:, None] - idx[None, :]
    lg = log_gamma[:, None, None]
    dmask = jnp.where(rel >= 0, jnp.exp(lg * jnp.maximum(rel, 0.0)), 0.0)
    scores = jnp.einsum('bhncd,bhnmd->bhncm', qc, kc) * dmask[None, :, None]
    inner = jnp.einsum('bhncm,bhnme->bhnce', scores, vc)
    zeta = jnp.exp(log_gamma[:, None] * (RET_CHUNK - 1 - idx)[None, :])
    kv = jnp.einsum('bhncd,bhnce->bhnde', kc * zeta[None, :, None, :, None], vc)
    chunk_decay = jnp.exp(log_gamma * RET_CHUNK)[None, :, None, None]

    def step(R, kv_n):
        return R * chunk_decay + kv_n, R

    _, R_prev = lax.scan(step, jnp.zeros_like(kv[:, :, 0]), kv.transpose(2, 0, 1, 3, 4))
    R_prev = R_prev.transpose(1, 2, 0, 3, 4)
    xi = jnp.exp(log_gamma[:, None] * (idx + 1.0)[None, :])
    cross = jnp.einsum('bhncd,bhnde->bhnce', qc, R_prev) * xi[None, :, None, :, None]
    out = inner + cross
    return out.transpose(0, 2, 3, 1, 4).reshape(B, S, H, vc.shape[-1])


def setup_inputs(seed: int = 0) -> dict:
    key = jax.random.key(seed)
    ks = jax.random.split(key, 20)
    D = D_MODEL
    nrm = lambda k, shape, fan_in, s=1.0: s * jax.random.normal(k, shape, jnp.float32) * fan_in ** -0.5
    x = jax.random.normal(ks[0], (BATCH, SEQ, D), jnp.float32)
    c = jax.random.normal(ks[1], (BATCH, D), jnp.float32)
    positions = jnp.broadcast_to(jnp.arange(SEQ, dtype=jnp.int32)[None, :], (BATCH, SEQ))
    return {
        "x": x,
        "c": c,
        "positions": positions,
        "ada_w": nrm(ks[2], (DEPTH, D, N_MOD * D), D, 0.1),
        "ada_b": 0.02 * jax.random.normal(ks[3], (DEPTH, N_MOD * D), jnp.float32),
        "norm_mix_g": 1.0 + 0.02 * jax.random.normal(ks[4], (DEPTH, D), jnp.float32),
        "w_in": nrm(ks[5], (DEPTH, D, IN_COLS), D),
        "conv_w": nrm(ks[6], (DEPTH, CONV_K, CONV_WIDTH), CONV_K),
        "ret_w_out": nrm(ks[7], (DEPTH, RET_WIDTH, D), RET_WIDTH),
        "conv_w_out": nrm(ks[8], (DEPTH, CONV_WIDTH, D), CONV_WIDTH),
        "mix_w_out": nrm(ks[9], (DEPTH, D, D), D),
        "norm_ffn_g": 1.0 + 0.02 * jax.random.normal(ks[10], (DEPTH, D), jnp.float32),
        "ffn_w_gate": nrm(ks[11], (DEPTH, D, FFN_HIDDEN), D),
        "ffn_w_up": nrm(ks[12], (DEPTH, D, FFN_HIDDEN), D),
        "ffn_w_down": nrm(ks[13], (DEPTH, FFN_HIDDEN, D), FFN_HIDDEN),
        "final_norm_g": 1.0 + 0.02 * jax.random.normal(ks[14], (D,), jnp.float32),
    }


def reference(x, c, positions, ada_w, ada_b, norm_mix_g, w_in, conv_w, ret_w_out,
              conv_w_out, mix_w_out, norm_ffn_g, ffn_w_gate, ffn_w_up, ffn_w_down,
              final_norm_g):
    B, S, D = x.shape
    H, d = N_RET_HEADS, RET_HEAD_DIM
    log_gamma = jnp.log(1.0 - 2.0 ** (-5.0 - jnp.arange(H, dtype=jnp.float32)))
    inv_freq = 1.0 / (ROPE_BASE ** (jnp.arange(0, d, 2, dtype=jnp.float32) / d))
    ang = positions.astype(jnp.float32)[..., None] * inv_freq
    cos, sin = jnp.cos(ang)[:, :, None, :], jnp.sin(ang)[:, :, None, :]
    split_pts = list(np.cumsum(IN_SPLITS)[:-1])
    cs = jax.nn.silu(c)

    for l in range(DEPTH):
        mod = cs @ ada_w[l] + ada_b[l]
        sh_m, sc_m, gt_m, sh_f, sc_f, gt_f = jnp.split(mod, N_MOD, axis=-1)

        h = modulate(rmsnorm(x, norm_mix_g[l]), sh_m, sc_m)
        proj = h @ w_in[l]
        q, k, v, g_ret, b_cv, c_cv, u_cv, g_a, g_b = jnp.split(proj, split_pts, axis=-1)

        q = rope(q.reshape(B, S, H, d), cos, sin)
        k = rope(k.reshape(B, S, H, d), cos, sin) * (d ** -0.5)
        v = v.reshape(B, S, H, d)
        y_ret = head_layernorm(retention_chunkwise(q, k, v, log_gamma)).reshape(B, S, RET_WIDTH)
        y_a = (jax.nn.silu(g_ret) * y_ret) @ ret_w_out[l]

        u = c_cv * u_cv
        u_pad = jnp.pad(u, ((0, 0), (CONV_K - 1, 0), (0, 0)))
        w = conv_w[l]
        conv = u_pad[:, 0:S] * w[0]
        for tap in range(1, CONV_K):
            conv = conv + u_pad[:, tap:tap + S] * w[tap]
        y_b = (b_cv * conv) @ conv_w_out[l]

        merged = jax.nn.sigmoid(g_a) * y_a + jax.nn.sigmoid(g_b) * y_b
        x = x + gt_m[:, None, :] * (merged @ mix_w_out[l])

        h = modulate(rmsnorm(x, norm_ffn_g[l]), sh_f, sc_f)
        f = (jax.nn.silu(h @ ffn_w_gate[l]) * (h @ ffn_w_up[l])) @ ffn_w_down[l]
        x = x + gt_f[:, None, :] * f

    return rmsnorm(x, final_norm_g)
```

```python
import functools

import jax
import jax.numpy as jnp
from jax.experimental import pallas as pl
from jax.experimental.pallas import tpu as pltpu

N_HEADS = 8
HEAD_DIM = 128
CHUNK = 128
CONV_K = 3
N_MOD = 6
EPS = 1e-6
ROPE_BASE = 10000.0
CONV_HALO = 8

MIX_ROWS = 256
FFN_ROWS = 512
MOD_COLS = 1536
V7X_VMEM_LIMIT = 56 * 1024 * 1024


def _resident(shape):
    nd = len(shape)
    return pl.BlockSpec(shape, lambda *_: (0,) * nd, pipeline_mode=pl.Buffered(1))


def _sigmoid(t):
    return 1.0 / (1.0 + jnp.exp(-t))


def _rmsnorm(x, g):
    return x * jax.lax.rsqrt(jnp.mean(x * x, axis=-1, keepdims=True) + EPS) * g


def _dot(a, b):
    return jnp.dot(a, b, preferred_element_type=jnp.float32)


def _adaln_kernel(c_ref, w_ref, b_ref, o_ref):
    c = c_ref[...]
    cs = (c * _sigmoid(c)).astype(jnp.bfloat16)
    o_ref[...] = _dot(cs, w_ref[...].astype(jnp.bfloat16)) + b_ref[...]


def _adaln_mod(c, ada_w, ada_b):
    B, D = c.shape
    N = ada_w.shape[1]
    return pl.pallas_call(
        _adaln_kernel,
        out_shape=jax.ShapeDtypeStruct((B, N), jnp.float32),
        grid=(N // MOD_COLS,),
        in_specs=[
            pl.BlockSpec((B, D), lambda j: (0, 0)),
            pl.BlockSpec((D, MOD_COLS), lambda j: (0, j)),
            pl.BlockSpec((1, MOD_COLS), lambda j: (0, j)),
        ],
        out_specs=pl.BlockSpec((B, MOD_COLS), lambda j: (0, j)),
        compiler_params=pltpu.CompilerParams(
            dimension_semantics=("arbitrary",), vmem_limit_bytes=V7X_VMEM_LIMIT),
        name="adaln_mod",
    )(c, ada_w, ada_b.reshape(1, N))


def _mixer_kernel(x_ref, pos_ref, mod_ref, g_ref, invf_ref, sign_ref,
                  dmask_ref, zeta_ref, xi_ref, cdec_ref,
                  w_in_ref, conv_w_ref, w_ret_ref, w_conv_ref, w_mix_ref,
                  o_ref,
                  proj_ref, state_ref, ubuf_ref, yret_ref):
    D = x_ref.shape[-1]
    TS = x_ref.shape[0]
    W = N_HEADS * HEAD_DIM

    @pl.when(pl.program_id(1) == 0)
    def _():
        state_ref[...] = jnp.zeros_like(state_ref)
        ubuf_ref[0:CONV_HALO, :] = jnp.zeros((CONV_HALO, D), jnp.float32)

    x = x_ref[...]
    shift, scale, gate = mod_ref[0:1, :], mod_ref[1:2, :], mod_ref[2:3, :]
    h = _rmsnorm(x, g_ref[...]) * (1.0 + scale) + shift
    proj_ref[...] = _dot(h.astype(jnp.bfloat16), w_in_ref[...])

    ang_t = invf_ref[...] * pos_ref[...].astype(jnp.float32)
    cos = jnp.cos(ang_t).T
    sin = (jnp.sin(ang_t) * sign_ref[...]).T
    k_scale = HEAD_DIM ** -0.5

    def rope(t, c, s):
        return t * c + pltpu.roll(t, HEAD_DIM // 2, 1) * s

    for hd in range(N_HEADS):
        col = hd * HEAD_DIM
        for ck in range(TS // CHUNK):
            rows = pl.ds(ck * CHUNK, CHUNK)
            c_ck, s_ck = cos[ck * CHUNK:(ck + 1) * CHUNK], sin[ck * CHUNK:(ck + 1) * CHUNK]
            q = rope(proj_ref[rows, pl.ds(col, HEAD_DIM)], c_ck, s_ck)
            k = rope(proj_ref[rows, pl.ds(W + col, HEAD_DIM)], c_ck, s_ck) * k_scale
            v = proj_ref[rows, pl.ds(2 * W + col, HEAD_DIM)]
            g_ret = proj_ref[rows, pl.ds(3 * W + col, HEAD_DIM)]
            qb, kb = q.astype(jnp.bfloat16), k.astype(jnp.bfloat16)
            scores = jax.lax.dot_general(qb, kb, (((1,), (1,)), ((), ())),
                                         preferred_element_type=jnp.float32) * dmask_ref[hd]
            r_prev = state_ref[hd]
            lhs = jnp.concatenate([scores.astype(jnp.bfloat16),
                                   (q * xi_ref[hd]).astype(jnp.bfloat16)], axis=1)
            rhs = jnp.concatenate([v.astype(jnp.bfloat16), r_prev.astype(jnp.bfloat16)], axis=0)
            out = _dot(lhs, rhs)
            kv = jax.lax.dot_general(kb, (v * zeta_ref[hd]).astype(jnp.bfloat16),
                                     (((0,), (0,)), ((), ())),
                                     preferred_element_type=jnp.float32)
            state_ref[hd] = r_prev * cdec_ref[hd] + kv
            mu = jnp.mean(out, axis=-1, keepdims=True)
            cen = out - mu
            var = jnp.mean(cen * cen, axis=-1, keepdims=True)
            y = cen * jax.lax.rsqrt(var + EPS)
            yret_ref[rows, pl.ds(col, HEAD_DIM)] = (g_ret * _sigmoid(g_ret) * y).astype(jnp.bfloat16)

    y_a = _dot(yret_ref[...], w_ret_ref[...])

    u = proj_ref[:, pl.ds(5 * W, D)] * proj_ref[:, pl.ds(6 * W, D)]
    ubuf_ref[CONV_HALO:CONV_HALO + TS, :] = u
    conv = u * conv_w_ref[CONV_K - 1:CONV_K, :]
    for tap in range(CONV_K - 1):
        back = CONV_K - 1 - tap
        conv = conv + ubuf_ref[CONV_HALO - back:CONV_HALO - back + TS, :] * conv_w_ref[tap:tap + 1, :]
    ubuf_ref[0:CONV_HALO, :] = ubuf_ref[TS:TS + CONV_HALO, :]
    y_b = _dot((proj_ref[:, pl.ds(4 * W, D)] * conv).astype(jnp.bfloat16), w_conv_ref[...])

    merged = (_sigmoid(proj_ref[:, pl.ds(7 * W, D)]) * y_a
              + _sigmoid(proj_ref[:, pl.ds(7 * W + D, D)]) * y_b)
    o_ref[...] = x + gate * _dot(merged.astype(jnp.bfloat16), w_mix_ref[...])


def _mixer(x, pos, mod, g, tables, w_in, conv_w, w_ret, w_conv, w_mix):
    B, S, D = x.shape
    TS = MIX_ROWS
    invf, sign, dmask, zeta, xi, cdec = tables
    n_in = w_in.shape[1]
    return pl.pallas_call(
        _mixer_kernel,
        out_shape=jax.ShapeDtypeStruct((B, S, D), jnp.float32),
        grid=(B, S // TS),
        in_specs=[
            pl.BlockSpec((None, TS, D), lambda b, s: (b, s, 0)),
            pl.BlockSpec((None, 1, TS), lambda b, s: (b, 0, s)),
            pl.BlockSpec((None, N_MOD, D), lambda b, s: (b, 0, 0)),
            _resident((1, D)),
            _resident(invf.shape), _resident(sign.shape),
            _resident(dmask.shape), _resident(zeta.shape), _resident(xi.shape), _resident(cdec.shape),
            _resident((D, n_in)), _resident(conv_w.shape),
            _resident(w_ret.shape), _resident(w_conv.shape), _resident(w_mix.shape),
        ],
        out_specs=pl.BlockSpec((None, TS, D), lambda b, s: (b, s, 0)),
        scratch_shapes=[
            pltpu.VMEM((TS, n_in), jnp.float32),
            pltpu.VMEM((N_HEADS, HEAD_DIM, HEAD_DIM), jnp.float32),
            pltpu.VMEM((TS + CONV_HALO, D), jnp.float32),
            pltpu.VMEM((TS, N_HEADS * HEAD_DIM), jnp.bfloat16),
        ],
        compiler_params=pltpu.CompilerParams(
            dimension_semantics=("arbitrary", "arbitrary"), vmem_limit_bytes=V7X_VMEM_LIMIT),
        name="mixer",
    )(x, pos, mod, g, invf, sign, dmask, zeta, xi, cdec, w_in, conv_w, w_ret, w_conv, w_mix)


def _ffn_kernel(x_ref, mod_ref, g_ref, gf_ref, wg_ref, wu_ref, wd_ref, o_ref):
    x = x_ref[...]
    shift, scale, gate = mod_ref[3:4, :], mod_ref[4:5, :], mod_ref[5:6, :]
    h = (_rmsnorm(x, g_ref[...]) * (1.0 + scale) + shift).astype(jnp.bfloat16)
    a = _dot(h, wg_ref[...])
    a = a * _sigmoid(a) * _dot(h, wu_ref[...])
    x2 = x + gate * _dot(a.astype(jnp.bfloat16), wd_ref[...])
    o_ref[...] = _rmsnorm(x2, gf_ref[...])


def _ffn(x, mod, g, gf, wg, wu, wd):
    B, S, D = x.shape
    TM = FFN_ROWS
    return pl.pallas_call(
        _ffn_kernel,
        out_shape=jax.ShapeDtypeStruct((B, S, D), jnp.float32),
        grid=(B, S // TM),
        in_specs=[
            pl.BlockSpec((None, TM, D), lambda b, s: (b, s, 0)),
            pl.BlockSpec((None, N_MOD, D), lambda b, s: (b, 0, 0)),
            _resident((1, D)), _resident((1, D)),
            _resident(wg.shape), _resident(wu.shape), _resident(wd.shape),
        ],
        out_specs=pl.BlockSpec((None, TM, D), lambda b, s: (b, s, 0)),
        compiler_params=pltpu.CompilerParams(
            dimension_semantics=("arbitrary", "arbitrary"), vmem_limit_bytes=V7X_VMEM_LIMIT),
        name="ffn",
    )(x, mod, g, gf, wg, wu, wd)


def _retention_tables(ts):
    H, d, C = N_HEADS, HEAD_DIM, CHUNK
    log_gamma = jnp.log(1.0 - 2.0 ** (-5.0 - jnp.arange(H, dtype=jnp.float32)))
    idx = jnp.arange(C, dtype=jnp.float32)
    rel = idx[:, None] - idx[None, :]
    lg = log_gamma[:, None, None]
    dmask = jnp.where(rel >= 0, jnp.exp(lg * jnp.maximum(rel, 0.0)), 0.0)
    zeta = jnp.exp(log_gamma[:, None] * (C - 1 - idx)[None, :])
    xi = jnp.exp(log_gamma[:, None] * (idx + 1.0)[None, :])
    cdec = jnp.exp(log_gamma * C)
    inv_freq = 1.0 / (ROPE_BASE ** (jnp.arange(0, d, 2, dtype=jnp.float32) / d))
    invf = jnp.broadcast_to(jnp.concatenate([inv_freq, inv_freq])[:, None], (d, ts))
    sign = jnp.broadcast_to(
        jnp.concatenate([-jnp.ones(d // 2, jnp.float32), jnp.ones(d // 2, jnp.float32)])[:, None], (d, ts))
    bc = lambda t: jnp.broadcast_to(t[:, :, None], (H, C, d))
    return (invf, sign, dmask, bc(zeta), bc(xi), jnp.broadcast_to(cdec[:, None, None], (H, 1, d)))


def kernel(x, c, positions, ada_w, ada_b, norm_mix_g, w_in, conv_w, ret_w_out, conv_w_out, mix_w_out,
           norm_ffn_g, ffn_w_gate, ffn_w_up, ffn_w_down, final_norm_g):
    B, S, D = x.shape
    depth = ada_w.shape[0]
    bf = lambda t: t.astype(jnp.bfloat16)
    tables = _retention_tables(MIX_ROWS)
    pos = positions.reshape(B, 1, S)
    for l in range(depth):
        mod = _adaln_mod(c, ada_w[l], ada_b[l]).reshape(B, N_MOD, D)
        x = _mixer(x, pos, mod, norm_mix_g[l].reshape(1, D), tables, bf(w_in[l]), conv_w[l],
                   bf(ret_w_out[l]), bf(conv_w_out[l]), bf(mix_w_out[l]))
        last = l == depth - 1
        assert last, "the final RMSNorm is fused into the last layer's ffn call"
        x = _ffn(x, mod, norm_ffn_g[l].reshape(1, D), final_norm_g.reshape(1, D),
                 bf(ffn_w_gate[l]), bf(ffn_w_up[l]), bf(ffn_w_down[l]))
    return x
```

```python
import functools

import jax
import jax.numpy as jnp
from jax.experimental import pallas as pl
from jax.experimental.pallas import tpu as pltpu

N_HEADS = 8
HEAD_DIM = 128
CHUNK = 128
CONV_K = 3
N_MOD = 6
EPS = 1e-6
ROPE_BASE = 10000.0
CONV_HALO = 8

MIX_ROWS = 256
FFN_ROWS = 512
MOD_COLS = 1536
V7X_VMEM_LIMIT = 56 * 1024 * 1024


def _resident(shape):
    nd = len(shape)
    return pl.BlockSpec(shape, lambda *_: (0,) * nd, pipeline_mode=pl.Buffered(1))


def _sigmoid(t):
    return 1.0 / (1.0 + jnp.exp(-t))


def _rmsnorm(x, g):
    return x * jax.lax.rsqrt(jnp.mean(x * x, axis=-1, keepdims=True) + EPS) * g


def _dot(a, b):
    return jnp.dot(a, b, preferred_element_type=jnp.float32)


def _adaln_kernel(c_ref, w_ref, b_ref, o_ref):
    c = c_ref[...]
    cs = (c * _sigmoid(c)).astype(jnp.bfloat16)
    o_ref[...] = _dot(cs, w_ref[...].astype(jnp.bfloat16)) + b_ref[...]


def _adaln_mod(c, ada_w, ada_b):
    B, D = c.shape
    N = ada_w.shape[1]
    return pl.pallas_call(
        _adaln_kernel,
        out_shape=jax.ShapeDtypeStruct((B, N), jnp.float32),
        grid=(N // MOD_COLS,),
        in_specs=[
            pl.BlockSpec((B, D), lambda j: (0, 0)),
            pl.BlockSpec((D, MOD_COLS), lambda j: (0, j)),
            pl.BlockSpec((1, MOD_COLS), lambda j: (0, j)),
        ],
        out_specs=pl.BlockSpec((B, MOD_COLS), lambda j: (0, j)),
        compiler_params=pltpu.CompilerParams(
            dimension_semantics=("arbitrary",), vmem_limit_bytes=V7X_VMEM_LIMIT),
        name="adaln_mod",
    )(c, ada_w, ada_b.reshape(1, N))


def _mixer_kernel(x_ref, pos_ref, mod_ref, g_ref, invf_ref,
                  dmask_ref, zeta_ref, xi_ref, cdec_ref,
                  w_in_ref, conv_w_ref, w_ret_ref, w_conv_ref, w_mix_ref,
                  o_ref,
                  state_ref, ubuf_ref, yret_ref):
    D = x_ref.shape[-1]
    TS = x_ref.shape[0]
    W = N_HEADS * HEAD_DIM
    PAIR = 2 * HEAD_DIM

    @pl.when(pl.program_id(1) == 0)
    def _():
        state_ref[...] = jnp.zeros_like(state_ref)
        ubuf_ref[0:CONV_HALO, :] = jnp.zeros((CONV_HALO, D), jnp.float32)

    x = x_ref[...]
    shift, scale, gate = mod_ref[0:1, :], mod_ref[1:2, :], mod_ref[2:3, :]
    h = (_rmsnorm(x, g_ref[...]) * (1.0 + scale) + shift).astype(jnp.bfloat16)

    def proj(col, width):
        return _dot(h, w_in_ref[:, pl.ds(col, width)])

    ang_t = invf_ref[...] * pos_ref[...].astype(jnp.float32)
    cos_h, sin_h = jnp.cos(ang_t), jnp.sin(ang_t)
    cos = jnp.concatenate([cos_h, cos_h], axis=0).T
    sin = jnp.concatenate([-sin_h, sin_h], axis=0).T
    k_scale = HEAD_DIM ** -0.5
    cos_k, sin_k = cos * k_scale, sin * k_scale

    def rope(t, c, s):
        return t * c + pltpu.roll(t, HEAD_DIM // 2, 1) * s

    def project_pair(hp):
        return [proj(sec * W + hp * PAIR, PAIR) for sec in range(4)]

    def score_stage(hp, qkvg):
        q2, k2, v2, g2 = qkvg
        units = []
        for j in range(2):
            hd = 2 * hp + j
            lanes = slice(j * HEAD_DIM, (j + 1) * HEAD_DIM)
            for ck in range(TS // CHUNK):
                rows = slice(ck * CHUNK, (ck + 1) * CHUNK)
                q = rope(q2[rows, lanes], cos[rows], sin[rows])
                k = rope(k2[rows, lanes], cos_k[rows], sin_k[rows])
                v = v2[rows, lanes]
                qb, kb = q.astype(jnp.bfloat16), k.astype(jnp.bfloat16)
                scores = jax.lax.dot_general(qb, kb, (((1,), (1,)), ((), ())),
                                             preferred_element_type=jnp.float32) * dmask_ref[hd]
                kv = jax.lax.dot_general(kb, (v * zeta_ref[hd]).astype(jnp.bfloat16),
                                         (((0,), (0,)), ((), ())),
                                         preferred_element_type=jnp.float32)
                lhs = jnp.concatenate([scores.astype(jnp.bfloat16),
                                       (q * xi_ref[hd]).astype(jnp.bfloat16)], axis=1)
                units.append((hd, ck, lhs, v.astype(jnp.bfloat16), kv, g2[rows, lanes]))
        return units

    def output_stage(units):
        for hd, ck, lhs, vb, kv, g_ret in units:
            r_prev = state_ref[hd]
            out = _dot(lhs, jnp.concatenate([vb, r_prev.astype(jnp.bfloat16)], axis=0))
            state_ref[hd] = r_prev * cdec_ref[hd] + kv
            mu = jnp.mean(out, axis=-1, keepdims=True)
            cen = out - mu
            var = jnp.mean(cen * cen, axis=-1, keepdims=True)
            y = cen * jax.lax.rsqrt(var + EPS)
            yret_ref[pl.ds(ck * CHUNK, CHUNK), pl.ds(hd * HEAD_DIM, HEAD_DIM)] = (
                g_ret * _sigmoid(g_ret) * y).astype(jnp.bfloat16)

    n_pairs = N_HEADS // 2
    fillers = [lambda: proj(5 * W, D), lambda: proj(6 * W, D), lambda: proj(4 * W, D)]
    filled = []
    pending = [project_pair(0), project_pair(1)]
    for hp in range(n_pairs):
        units = score_stage(hp, pending.pop(0))
        if hp + 2 < n_pairs:
            pending.append(project_pair(hp + 2))
        else:
            filled.append(fillers.pop(0)())
        output_stage(units)
    filled.append(fillers.pop(0)())
    c_cv, u_cv, b_cv = filled

    u = c_cv * u_cv
    ubuf_ref[CONV_HALO:CONV_HALO + TS, :] = u
    conv = u * conv_w_ref[CONV_K - 1:CONV_K, :]
    for tap in range(CONV_K - 1):
        back = CONV_K - 1 - tap
        conv = conv + ubuf_ref[CONV_HALO - back:CONV_HALO - back + TS, :] * conv_w_ref[tap:tap + 1, :]
    ubuf_ref[0:CONV_HALO, :] = ubuf_ref[TS:TS + CONV_HALO, :]
    g_a = proj(7 * W, D)
    y_a = _dot(yret_ref[...], w_ret_ref[...])
    g_b = proj(7 * W + D, D)
    y_b = _dot((b_cv * conv).astype(jnp.bfloat16), w_conv_ref[...])

    merged = _sigmoid(g_a) * y_a + _sigmoid(g_b) * y_b
    o_ref[...] = x + gate * _dot(merged.astype(jnp.bfloat16), w_mix_ref[...])


def _mixer(x, pos, mod, g, tables, w_in, conv_w, w_ret, w_conv, w_mix):
    B, S, D = x.shape
    TS = MIX_ROWS
    invf, dmask, zeta, xi, cdec = tables
    n_in = w_in.shape[1]
    return pl.pallas_call(
        _mixer_kernel,
        out_shape=jax.ShapeDtypeStruct((B, S, D), jnp.float32),
        grid=(B, S // TS),
        in_specs=[
            pl.BlockSpec((None, TS, D), lambda b, s: (b, s, 0)),
            pl.BlockSpec((None, 1, TS), lambda b, s: (b, 0, s)),
            pl.BlockSpec((None, N_MOD, D), lambda b, s: (b, 0, 0)),
            _resident((1, D)),
            _resident(invf.shape),
            _resident(dmask.shape), _resident(zeta.shape), _resident(xi.shape), _resident(cdec.shape),
            _resident((D, n_in)), _resident(conv_w.shape),
            _resident(w_ret.shape), _resident(w_conv.shape), _resident(w_mix.shape),
        ],
        out_specs=pl.BlockSpec((None, TS, D), lambda b, s: (b, s, 0)),
        scratch_shapes=[
            pltpu.VMEM((N_HEADS, HEAD_DIM, HEAD_DIM), jnp.float32),
            pltpu.VMEM((TS + CONV_HALO, D), jnp.float32),
            pltpu.VMEM((TS, N_HEADS * HEAD_DIM), jnp.bfloat16),
        ],
        compiler_params=pltpu.CompilerParams(
            dimension_semantics=("arbitrary", "arbitrary"), vmem_limit_bytes=V7X_VMEM_LIMIT),
        name="mixer",
    )(x, pos, mod, g, invf, dmask, zeta, xi, cdec, w_in, conv_w, w_ret, w_conv, w_mix)


def _ffn_kernel(x_ref, mod_ref, g_ref, gf_ref, wg_ref, wu_ref, wd_ref, o_ref):
    x = x_ref[...]
    shift, scale, gate = mod_ref[3:4, :], mod_ref[4:5, :], mod_ref[5:6, :]
    h = (_rmsnorm(x, g_ref[...]) * (1.0 + scale) + shift).astype(jnp.bfloat16)
    a = _dot(h, wg_ref[...])
    a = a * _sigmoid(a) * _dot(h, wu_ref[...])
    x2 = x + gate * _dot(a.astype(jnp.bfloat16), wd_ref[...])
    o_ref[...] = _rmsnorm(x2, gf_ref[...])


def _ffn(x, mod, g, gf, wg, wu, wd):
    B, S, D = x.shape
    TM = FFN_ROWS
    return pl.pallas_call(
        _ffn_kernel,
        out_shape=jax.ShapeDtypeStruct((B, S, D), jnp.float32),
        grid=(B, S // TM),
        in_specs=[
            pl.BlockSpec((None, TM, D), lambda b, s: (b, s, 0)),
            pl.BlockSpec((None, N_MOD, D), lambda b, s: (b, 0, 0)),
            _resident((1, D)), _resident((1, D)),
            _resident(wg.shape), _resident(wu.shape), _resident(wd.shape),
        ],
        out_specs=pl.BlockSpec((None, TM, D), lambda b, s: (b, s, 0)),
        compiler_params=pltpu.CompilerParams(
            dimension_semantics=("arbitrary", "arbitrary"), vmem_limit_bytes=V7X_VMEM_LIMIT),
        name="ffn",
    )(x, mod, g, gf, wg, wu, wd)


def _retention_tables(ts):
    H, d, C = N_HEADS, HEAD_DIM, CHUNK
    log_gamma = jnp.log(1.0 - 2.0 ** (-5.0 - jnp.arange(H, dtype=jnp.float32)))
    idx = jnp.arange(C, dtype=jnp.float32)
    rel = idx[:, None] - idx[None, :]
    lg = log_gamma[:, None, None]
    dmask = jnp.where(rel >= 0, jnp.exp(lg * jnp.maximum(rel, 0.0)), 0.0)
    zeta = jnp.exp(log_gamma[:, None] * (C - 1 - idx)[None, :])
    xi = jnp.exp(log_gamma[:, None] * (idx + 1.0)[None, :])
    cdec = jnp.exp(log_gamma * C)
    inv_freq = 1.0 / (ROPE_BASE ** (jnp.arange(0, d, 2, dtype=jnp.float32) / d))
    invf = jnp.broadcast_to(inv_freq[:, None], (d // 2, ts))
    bc = lambda t: jnp.broadcast_to(t[:, :, None], (H, C, d))
    return (invf, dmask, bc(zeta), bc(xi), jnp.broadcast_to(cdec[:, None, None], (H, 1, d)))


def kernel(x, c, positions, ada_w, ada_b, norm_mix_g, w_in, conv_w, ret_w_out, conv_w_out, mix_w_out,
           norm_ffn_g, ffn_w_gate, ffn_w_up, ffn_w_down, final_norm_g):
    B, S, D = x.shape
    depth = ada_w.shape[0]
    bf = lambda t: t.astype(jnp.bfloat16)
    tables = _retention_tables(MIX_ROWS)
    pos = positions.reshape(B, 1, S)
    for l in range(depth):
        mod = _adaln_mod(c, ada_w[l], ada_b[l]).reshape(B, N_MOD, D)
        x = _mixer(x, pos, mod, norm_mix_g[l].reshape(1, D), tables, bf(w_in[l]), conv_w[l],
                   bf(ret_w_out[l]), bf(conv_w_out[l]), bf(mix_w_out[l]))
        last = l == depth - 1
        assert last, "the final RMSNorm is fused into the last layer's ffn call"
        x = _ffn(x, mod, norm_ffn_g[l].reshape(1, D), final_norm_g.reshape(1, D),
                 bf(ffn_w_gate[l]), bf(ffn_w_up[l]), bf(ffn_w_down[l]))
    return x
```

```python
import jax
import jax.numpy as jnp
from jax.experimental import pallas as pl
from jax.experimental.pallas import tpu as pltpu

N_HEADS = 8
HEAD_DIM = 128
CHUNK = 128
CONV_K = 3
N_MOD = 6
EPS = 1e-6
ROPE_BASE = 10000.0
CONV_HALO = 8

MIX_ROWS = 512
MIX_SUB_ROWS = 256
FFN_ROWS = 1024
FFN_SUB_ROWS = 256
MOD_COLS = 1536
V7X_VMEM_LIMIT = 56 * 1024 * 1024


def _resident(shape):
    nd = len(shape)
    return pl.BlockSpec(shape, lambda *_: (0,) * nd, pipeline_mode=pl.Buffered(1))


def _sigmoid(t):
    return 1.0 / (1.0 + jnp.exp(-t))


def _rmsnorm(x, g):
    return x * jax.lax.rsqrt(jnp.mean(x * x, axis=-1, keepdims=True) + EPS) * g


def _dot(a, b):
    return jnp.dot(a, b, preferred_element_type=jnp.float32)


def _adaln_kernel(c_ref, w_ref, b_ref, o_ref):
    c = c_ref[...]
    cs = (c * _sigmoid(c)).astype(jnp.bfloat16)
    o_ref[...] = _dot(cs, w_ref[...].astype(jnp.bfloat16)) + b_ref[...]


def _adaln_mod(c, ada_w, ada_b):
    B, D = c.shape
    N = ada_w.shape[1]
    return pl.pallas_call(
        _adaln_kernel,
        out_shape=jax.ShapeDtypeStruct((B, N), jnp.float32),
        grid=(N // MOD_COLS,),
        in_specs=[
            pl.BlockSpec((B, D), lambda j: (0, 0)),
            pl.BlockSpec((D, MOD_COLS), lambda j: (0, j)),
            pl.BlockSpec((1, MOD_COLS), lambda j: (0, j)),
        ],
        out_specs=pl.BlockSpec((B, MOD_COLS), lambda j: (0, j)),
        compiler_params=pltpu.CompilerParams(
            dimension_semantics=("arbitrary",), vmem_limit_bytes=V7X_VMEM_LIMIT),
        name="adaln_mod",
    )(c, ada_w, ada_b.reshape(1, N))


def _mixer_kernel(x_ref, pos_ref, mod_ref, g_ref, invf_ref,
                  dmask_ref, zeta_ref, xi_ref, cdec_ref,
                  w_in_ref, conv_w_ref, w_ret_ref, w_conv_ref, w_mix_ref,
                  o_ref,
                  state_ref, ubuf_ref, yret_ref):
    D = x_ref.shape[-1]
    TS = x_ref.shape[0]
    SUB = MIX_SUB_ROWS
    W = N_HEADS * HEAD_DIM
    PAIR = 2 * HEAD_DIM
    n_pairs = N_HEADS // 2

    @pl.when(pl.program_id(1) == 0)
    def _():
        state_ref[...] = jnp.zeros_like(state_ref)
        ubuf_ref[0:CONV_HALO, :] = jnp.zeros((CONV_HALO, D), jnp.float32)

    shift, scale, gate = mod_ref[0:1, :], mod_ref[1:2, :], mod_ref[2:3, :]

    ang_t = invf_ref[...] * pos_ref[...].astype(jnp.float32)
    cos_h, sin_h = jnp.cos(ang_t), jnp.sin(ang_t)
    cos = jnp.concatenate([cos_h, cos_h], axis=0).T
    sin = jnp.concatenate([-sin_h, sin_h], axis=0).T
    k_scale = HEAD_DIM ** -0.5
    cos_k, sin_k = cos * k_scale, sin * k_scale

    def rope(t, c, s):
        return t * c + pltpu.roll(t, HEAD_DIM // 2, 1) * s

    def sub_tile(row0):
        x = x_ref[pl.ds(row0, SUB), :]
        h = (_rmsnorm(x, g_ref[...]) * (1.0 + scale) + shift).astype(jnp.bfloat16)

        def proj(col, width):
            return _dot(h, w_in_ref[:, pl.ds(col, width)])

        def project_pair(hp):
            return [proj(sec * W + hp * PAIR, PAIR) for sec in range(4)]

        def score_stage(hp, qkvg):
            q2, k2, v2, g2 = qkvg
            units = []
            for j in range(2):
                hd = 2 * hp + j
                lanes = slice(j * HEAD_DIM, (j + 1) * HEAD_DIM)
                for ck in range(SUB // CHUNK):
                    rows = slice(ck * CHUNK, (ck + 1) * CHUNK)
                    trows = slice(row0 + ck * CHUNK, row0 + (ck + 1) * CHUNK)
                    q = rope(q2[rows, lanes], cos[trows], sin[trows])
                    k = rope(k2[rows, lanes], cos_k[trows], sin_k[trows])
                    v = v2[rows, lanes]
                    qb, kb = q.astype(jnp.bfloat16), k.astype(jnp.bfloat16)
                    scores = jax.lax.dot_general(qb, kb, (((1,), (1,)), ((), ())),
                                                 preferred_element_type=jnp.float32) * dmask_ref[hd]
                    kv = jax.lax.dot_general(kb, (v * zeta_ref[hd]).astype(jnp.bfloat16),
                                             (((0,), (0,)), ((), ())),
                                             preferred_element_type=jnp.float32)
                    lhs = jnp.concatenate([scores.astype(jnp.bfloat16),
                                           (q * xi_ref[hd]).astype(jnp.bfloat16)], axis=1)
                    units.append((hd, row0 + ck * CHUNK, lhs, v.astype(jnp.bfloat16), kv, g2[rows, lanes]))
            return units

        def output_stage(units):
            for hd, trow, lhs, vb, kv, g_ret in units:
                r_prev = state_ref[hd]
                out = _dot(lhs, jnp.concatenate([vb, r_prev.astype(jnp.bfloat16)], axis=0))
                state_ref[hd] = r_prev * cdec_ref[hd] + kv
                mu = jnp.mean(out, axis=-1, keepdims=True)
                cen = out - mu
                var = jnp.mean(cen * cen, axis=-1, keepdims=True)
                y = cen * jax.lax.rsqrt(var + EPS)
                yret_ref[pl.ds(trow, CHUNK), pl.ds(hd * HEAD_DIM, HEAD_DIM)] = (
                    g_ret * _sigmoid(g_ret) * y).astype(jnp.bfloat16)

        fillers = [lambda: proj(5 * W, D), lambda: proj(6 * W, D), lambda: proj(4 * W, D)]
        filled = []
        pending = [project_pair(0), project_pair(1)]
        for hp in range(n_pairs):
            units = score_stage(hp, pending.pop(0))
            if hp + 2 < n_pairs:
                pending.append(project_pair(hp + 2))
            else:
                filled.append(fillers.pop(0)())
            output_stage(units)
        filled.append(fillers.pop(0)())
        c_cv, u_cv, b_cv = filled

        u = c_cv * u_cv
        base = CONV_HALO + row0
        ubuf_ref[pl.ds(base, SUB), :] = u
        conv = u * conv_w_ref[CONV_K - 1:CONV_K, :]
        for tap in range(CONV_K - 1):
            back = CONV_K - 1 - tap
            conv = conv + ubuf_ref[pl.ds(base - back, SUB), :] * conv_w_ref[tap:tap + 1, :]
        g_a = proj(7 * W, D)
        y_a = _dot(yret_ref[pl.ds(row0, SUB), :], w_ret_ref[...])
        g_b = proj(7 * W + D, D)
        y_b = _dot((b_cv * conv).astype(jnp.bfloat16), w_conv_ref[...])

        merged = _sigmoid(g_a) * y_a + _sigmoid(g_b) * y_b
        o_ref[pl.ds(row0, SUB), :] = x + gate * _dot(merged.astype(jnp.bfloat16), w_mix_ref[...])

    for sub in range(TS // SUB):
        sub_tile(sub * SUB)
    ubuf_ref[0:CONV_HALO, :] = ubuf_ref[TS:TS + CONV_HALO, :]


def _mixer(x, pos, mod, g, tables, w_in, conv_w, w_ret, w_conv, w_mix):
    B, S, D = x.shape
    TS = MIX_ROWS
    invf, dmask, zeta, xi, cdec = tables
    n_in = w_in.shape[1]
    return pl.pallas_call(
        _mixer_kernel,
        out_shape=jax.ShapeDtypeStruct((B, S, D), jnp.float32),
        grid=(B, S // TS),
        in_specs=[
            pl.BlockSpec((None, TS, D), lambda b, s: (b, s, 0)),
            pl.BlockSpec((None, 1, TS), lambda b, s: (b, 0, s)),
            pl.BlockSpec((None, N_MOD, D), lambda b, s: (b, 0, 0)),
            _resident((1, D)),
            _resident(invf.shape),
            _resident(dmask.shape), _resident(zeta.shape), _resident(xi.shape), _resident(cdec.shape),
            _resident((D, n_in)), _resident(conv_w.shape),
            _resident(w_ret.shape), _resident(w_conv.shape), _resident(w_mix.shape),
        ],
        out_specs=pl.BlockSpec((None, TS, D), lambda b, s: (b, s, 0)),
        scratch_shapes=[
            pltpu.VMEM((N_HEADS, HEAD_DIM, HEAD_DIM), jnp.float32),
            pltpu.VMEM((TS + CONV_HALO, D), jnp.float32),
            pltpu.VMEM((TS, N_HEADS * HEAD_DIM), jnp.bfloat16),
        ],
        compiler_params=pltpu.CompilerParams(
            dimension_semantics=("arbitrary", "arbitrary"), vmem_limit_bytes=V7X_VMEM_LIMIT),
        name="mixer",
    )(x, pos, mod, g, invf, dmask, zeta, xi, cdec, w_in, conv_w, w_ret, w_conv, w_mix)


def _ffn_kernel(x_ref, mod_ref, g_ref, gf_ref, wg_ref, wu_ref, wd_ref, o_ref):
    shift, scale, gate = mod_ref[3:4, :], mod_ref[4:5, :], mod_ref[5:6, :]
    n_sub = x_ref.shape[0] // FFN_SUB_ROWS

    def hidden(i):
        x = x_ref[pl.ds(i * FFN_SUB_ROWS, FFN_SUB_ROWS), :]
        h = (_rmsnorm(x, g_ref[...]) * (1.0 + scale) + shift).astype(jnp.bfloat16)
        a = _dot(h, wg_ref[...])
        return (a * _sigmoid(a) * _dot(h, wu_ref[...])).astype(jnp.bfloat16)

    def finish(i, a):
        rows = pl.ds(i * FFN_SUB_ROWS, FFN_SUB_ROWS)
        x2 = x_ref[rows, :] + gate * _dot(a, wd_ref[...])
        o_ref[rows, :] = _rmsnorm(x2, gf_ref[...])

    a_prev = hidden(0)
    for i in range(1, n_sub):
        a_next = hidden(i)
        finish(i - 1, a_prev)
        a_prev = a_next
    finish(n_sub - 1, a_prev)


def _ffn(x, mod, g, gf, wg, wu, wd):
    B, S, D = x.shape
    TM = FFN_ROWS
    return pl.pallas_call(
        _ffn_kernel,
        out_shape=jax.ShapeDtypeStruct((B, S, D), jnp.float32),
        grid=(B, S // TM),
        in_specs=[
            pl.BlockSpec((None, TM, D), lambda b, s: (b, s, 0)),
            pl.BlockSpec((None, N_MOD, D), lambda b, s: (b, 0, 0)),
            _resident((1, D)), _resident((1, D)),
            _resident(wg.shape), _resident(wu.shape), _resident(wd.shape),
        ],
        out_specs=pl.BlockSpec((None, TM, D), lambda b, s: (b, s, 0)),
        compiler_params=pltpu.CompilerParams(
            dimension_semantics=("arbitrary", "arbitrary"), vmem_limit_bytes=V7X_VMEM_LIMIT),
        name="ffn",
    )(x, mod, g, gf, wg, wu, wd)


def _retention_tables(ts):
    H, d, C = N_HEADS, HEAD_DIM, CHUNK
    log_gamma = jnp.log(1.0 - 2.0 ** (-5.0 - jnp.arange(H, dtype=jnp.float32)))
    idx = jnp.arange(C, dtype=jnp.float32)
    rel = idx[:, None] - idx[None, :]
    lg = log_gamma[:, None, None]
    dmask = jnp.where(rel >= 0, jnp.exp(lg * jnp.maximum(rel, 0.0)), 0.0)
    zeta = jnp.exp(log_gamma[:, None] * (C - 1 - idx)[None, :])
    xi = jnp.exp(log_gamma[:, None] * (idx + 1.0)[None, :])
    cdec = jnp.exp(log_gamma * C)
    inv_freq = 1.0 / (ROPE_BASE ** (jnp.arange(0, d, 2, dtype=jnp.float32) / d))
    invf = jnp.broadcast_to(inv_freq[:, None], (d // 2, ts))
    bc = lambda t: jnp.broadcast_to(t[:, :, None], (H, C, d))
    return (invf, dmask, bc(zeta), bc(xi), jnp.broadcast_to(cdec[:, None, None], (H, 1, d)))


def kernel(x, c, positions, ada_w, ada_b, norm_mix_g, w_in, conv_w, ret_w_out, conv_w_out, mix_w_out,
           norm_ffn_g, ffn_w_gate, ffn_w_up, ffn_w_down, final_norm_g):
    B, S, D = x.shape
    depth = ada_w.shape[0]
    assert depth == 1, "the final RMSNorm is fused into the (single) layer's ffn call"
    bf = lambda t: t.astype(jnp.bfloat16)
    tables = _retention_tables(MIX_ROWS)
    pos = positions.reshape(B, 1, S)
    mod = _adaln_mod(c, ada_w[0], ada_b[0]).reshape(B, N_MOD, D)
    x = _mixer(x, pos, mod, norm_mix_g[0].reshape(1, D), tables, bf(w_in[0]), conv_w[0],
               bf(ret_w_out[0]), bf(conv_w_out[0]), bf(mix_w_out[0]))
    return _ffn(x, mod, norm_ffn_g[0].reshape(1, D), final_norm_g.reshape(1, D),
                bf(ffn_w_gate[0]), bf(ffn_w_up[0]), bf(ffn_w_down[0]))
```

```python
import functools

import jax
import jax.numpy as jnp
import numpy as np
from jax.experimental import pallas as pl
from jax.experimental.pallas import tpu as pltpu

N_HEADS = 8
HEAD_DIM = 128
CHUNK = 128
CONV_K = 3
N_MOD = 6
EPS = 1e-6
ROPE_BASE = 10000.0
CONV_HALO = 8

MIX_ROWS = 512
MIX_SUB_ROWS = 256
FFN_ROWS = 1024
FFN_SUB_ROWS = 256
MOD_COLS = 1536
FFN_STAGE_COLS = 1408
CAST_BUFFERS = 2
V7X_VMEM_LIMIT = 56 * 1024 * 1024


def _resident(shape):
    nd = len(shape)
    return pl.BlockSpec(shape, lambda *_: (0,) * nd, pipeline_mode=pl.Buffered(1))


def _sigmoid(t):
    return 1.0 / (1.0 + jnp.exp(-t))


def _rmsnorm(x, g):
    return x * jax.lax.rsqrt(jnp.mean(x * x, axis=-1, keepdims=True) + EPS) * g


def _dot(a, b):
    return jnp.dot(a, b, preferred_element_type=jnp.float32)


def _hbm():
    return pl.BlockSpec(memory_space=pl.ANY)


def _is_first_step():
    return (pl.program_id(0) == 0) & (pl.program_id(1) == 0)


def _stream_cast(chunks, stage_ref, sem_ref):
    n_buf = stage_ref.shape[0]

    def copy(i):
        rows, cols = chunks[i][0].shape
        slot = i % n_buf
        return pltpu.make_async_copy(chunks[i][0], stage_ref.at[slot, pl.ds(0, rows), pl.ds(0, cols)],
                                     sem_ref.at[slot])

    for i in range(min(n_buf, len(chunks))):
        copy(i).start()
    for i, (src, dst) in enumerate(chunks):
        rows, cols = src.shape
        copy(i).wait()
        dst[...] = stage_ref[i % n_buf, pl.ds(0, rows), pl.ds(0, cols)].astype(jnp.bfloat16)
        if i + n_buf < len(chunks):
            copy(i + n_buf).start()


def _col_chunks(src, dst, width):
    return [(src.at[:, pl.ds(c, width)], dst.at[:, pl.ds(c, width)]) for c in range(0, src.shape[1], width)]


def _row_chunks(src, dst, height):
    return [(src.at[pl.ds(r, min(height, src.shape[0] - r)), :], dst.at[pl.ds(r, min(height, src.shape[0] - r)), :])
            for r in range(0, src.shape[0], height)]


def _adaln_kernel(c_ref, w_hbm, b_ref, o_ref, w_ref, sem_ref):
    n_chunks = w_ref.shape[1] // MOD_COLS
    copies = [pltpu.make_async_copy(w_hbm.at[:, pl.ds(j * MOD_COLS, MOD_COLS)],
                                    w_ref.at[:, pl.ds(j * MOD_COLS, MOD_COLS)], sem_ref.at[j])
              for j in range(n_chunks)]
    for cp in copies:
        cp.start()
    c = c_ref[...]
    cs = (c * _sigmoid(c)).astype(jnp.bfloat16)
    for j, cp in enumerate(copies):
        cols = pl.ds(j * MOD_COLS, MOD_COLS)
        cp.wait()
        o_ref[:, cols] = _dot(cs, w_ref[:, cols].astype(jnp.bfloat16)) + b_ref[:, cols]


def _adaln_mod(c, ada_w, ada_b):
    B, D = c.shape
    N = ada_w.shape[1]
    vmem = pl.BlockSpec(memory_space=pltpu.VMEM)
    return pl.pallas_call(
        _adaln_kernel,
        out_shape=jax.ShapeDtypeStruct((B, N), jnp.float32),
        in_specs=[vmem, _hbm(), vmem],
        out_specs=vmem,
        scratch_shapes=[pltpu.VMEM((D, N), jnp.float32), pltpu.SemaphoreType.DMA((N // MOD_COLS,))],
        compiler_params=pltpu.CompilerParams(vmem_limit_bytes=V7X_VMEM_LIMIT),
        name="adaln_mod",
    )(c, ada_w, ada_b.reshape(1, N))


def _mixer_kernel(x_ref, pos_ref, mod_ref, g_ref, invf_ref,
                  dmask_ref, zeta_ref, xi_ref, cdec_ref,
                  conv_w_ref, w_in_hbm, w_ret_hbm, w_conv_hbm, w_mix_hbm,
                  o_ref,
                  state_ref, ubuf_ref, yret_ref,
                  w_in_ref, w_ret_ref, w_conv_ref, w_mix_ref, stage_ref, sem_ref):
    D = x_ref.shape[-1]
    TS = x_ref.shape[0]
    SUB = MIX_SUB_ROWS
    W = N_HEADS * HEAD_DIM
    PAIR = 2 * HEAD_DIM
    n_pairs = N_HEADS // 2

    @pl.when(_is_first_step())
    def _():
        _stream_cast(_col_chunks(w_in_hbm, w_in_ref, D)
                     + [(w_ret_hbm, w_ret_ref), (w_conv_hbm, w_conv_ref), (w_mix_hbm, w_mix_ref)],
                     stage_ref, sem_ref)

    @pl.when(pl.program_id(1) == 0)
    def _():
        state_ref[...] = jnp.zeros_like(state_ref)
        ubuf_ref[0:CONV_HALO, :] = jnp.zeros((CONV_HALO, D), jnp.float32)

    shift, scale, gate = mod_ref[0:1, :], mod_ref[1:2, :], mod_ref[2:3, :]

    ang_t = invf_ref[...] * pos_ref[...].astype(jnp.float32)
    cos_h, sin_h = jnp.cos(ang_t), jnp.sin(ang_t)
    cos = jnp.concatenate([cos_h, cos_h], axis=0).T
    sin = jnp.concatenate([-sin_h, sin_h], axis=0).T
    k_scale = HEAD_DIM ** -0.5
    cos_k, sin_k = cos * k_scale, sin * k_scale

    def rope(t, c, s):
        return t * c + pltpu.roll(t, HEAD_DIM // 2, 1) * s

    def sub_tile(row0):
        x = x_ref[pl.ds(row0, SUB), :]
        h = (_rmsnorm(x, g_ref[...]) * (1.0 + scale) + shift).astype(jnp.bfloat16)

        def proj(col, width):
            return _dot(h, w_in_ref[:, pl.ds(col, width)])

        def project_pair(hp):
            return [proj(sec * W + hp * PAIR, PAIR) for sec in range(4)]

        def score_stage(hp, qkvg):
            q2, k2, v2, g2 = qkvg
            units = []
            for j in range(2):
                hd = 2 * hp + j
                lanes = slice(j * HEAD_DIM, (j + 1) * HEAD_DIM)
                for ck in range(SUB // CHUNK):
                    rows = slice(ck * CHUNK, (ck + 1) * CHUNK)
                    trows = slice(row0 + ck * CHUNK, row0 + (ck + 1) * CHUNK)
                    q = rope(q2[rows, lanes], cos[trows], sin[trows])
                    k = rope(k2[rows, lanes], cos_k[trows], sin_k[trows])
                    v = v2[rows, lanes]
                    qb, kb = q.astype(jnp.bfloat16), k.astype(jnp.bfloat16)
                    scores = jax.lax.dot_general(qb, kb, (((1,), (1,)), ((), ())),
                                                 preferred_element_type=jnp.float32) * dmask_ref[hd]
                    kv = jax.lax.dot_general(kb, (v * zeta_ref[hd]).astype(jnp.bfloat16),
                                             (((0,), (0,)), ((), ())),
                                             preferred_element_type=jnp.float32)
                    lhs = jnp.concatenate([scores.astype(jnp.bfloat16),
                                           (q * xi_ref[hd]).astype(jnp.bfloat16)], axis=1)
                    units.append((hd, row0 + ck * CHUNK, lhs, v.astype(jnp.bfloat16), kv, g2[rows, lanes]))
            return units

        def output_stage(units):
            for hd, trow, lhs, vb, kv, g_ret in units:
                r_prev = state_ref[hd]
                out = _dot(lhs, jnp.concatenate([vb, r_prev.astype(jnp.bfloat16)], axis=0))
                state_ref[hd] = r_prev * cdec_ref[hd] + kv
                mu = jnp.mean(out, axis=-1, keepdims=True)
                cen = out - mu
                var = jnp.mean(cen * cen, axis=-1, keepdims=True)
                y = cen * jax.lax.rsqrt(var + EPS)
                yret_ref[pl.ds(trow, CHUNK), pl.ds(hd * HEAD_DIM, HEAD_DIM)] = (
                    g_ret * _sigmoid(g_ret) * y).astype(jnp.bfloat16)

        fillers = [lambda: proj(5 * W, D), lambda: proj(6 * W, D), lambda: proj(4 * W, D)]
        filled = []
        pending = [project_pair(0), project_pair(1)]
        for hp in range(n_pairs):
            units = score_stage(hp, pending.pop(0))
            if hp + 2 < n_pairs:
                pending.append(project_pair(hp + 2))
            else:
                filled.append(fillers.pop(0)())
            output_stage(units)
        filled.append(fillers.pop(0)())
        c_cv, u_cv, b_cv = filled

        u = c_cv * u_cv
        base = CONV_HALO + row0
        ubuf_ref[pl.ds(base, SUB), :] = u
        conv = u * conv_w_ref[CONV_K - 1:CONV_K, :]
        for tap in range(CONV_K - 1):
            back = CONV_K - 1 - tap
            conv = conv + ubuf_ref[pl.ds(base - back, SUB), :] * conv_w_ref[tap:tap + 1, :]
        g_a = proj(7 * W, D)
        y_a = _dot(yret_ref[pl.ds(row0, SUB), :], w_ret_ref[...])
        g_b = proj(7 * W + D, D)
        y_b = _dot((b_cv * conv).astype(jnp.bfloat16), w_conv_ref[...])

        merged = _sigmoid(g_a) * y_a + _sigmoid(g_b) * y_b
        o_ref[pl.ds(row0, SUB), :] = x + gate * _dot(merged.astype(jnp.bfloat16), w_mix_ref[...])

    for sub in range(TS // SUB):
        sub_tile(sub * SUB)
    ubuf_ref[0:CONV_HALO, :] = ubuf_ref[TS:TS + CONV_HALO, :]


def _mixer(x, pos, mod, g, tables, w_in, conv_w, w_ret, w_conv, w_mix):
    B, S, D = x.shape
    TS = MIX_ROWS
    invf, dmask, zeta, xi, cdec = tables
    n_in = w_in.shape[1]
    return pl.pallas_call(
        _mixer_kernel,
        out_shape=jax.ShapeDtypeStruct((B, S, D), jnp.float32),
        grid=(B, S // TS),
        in_specs=[
            pl.BlockSpec((None, TS, D), lambda b, s: (b, s, 0)),
            pl.BlockSpec((None, 1, TS), lambda b, s: (b, 0, s)),
            pl.BlockSpec((None, N_MOD, D), lambda b, s: (b, 0, 0)),
            _resident((1, D)),
            _resident(invf.shape),
            _resident(dmask.shape), _resident(zeta.shape), _resident(xi.shape), _resident(cdec.shape),
            _resident(conv_w.shape), _hbm(), _hbm(), _hbm(), _hbm(),
        ],
        out_specs=pl.BlockSpec((None, TS, D), lambda b, s: (b, s, 0)),
        scratch_shapes=[
            pltpu.VMEM((N_HEADS, HEAD_DIM, HEAD_DIM), jnp.float32),
            pltpu.VMEM((TS + CONV_HALO, D), jnp.float32),
            pltpu.VMEM((TS, N_HEADS * HEAD_DIM), jnp.bfloat16),
            pltpu.VMEM((D, n_in), jnp.bfloat16),
            pltpu.VMEM((D, D), jnp.bfloat16), pltpu.VMEM((D, D), jnp.bfloat16), pltpu.VMEM((D, D), jnp.bfloat16),
            pltpu.VMEM((CAST_BUFFERS, D, D), jnp.float32), pltpu.SemaphoreType.DMA((CAST_BUFFERS,)),
        ],
        compiler_params=pltpu.CompilerParams(
            dimension_semantics=("arbitrary", "arbitrary"), vmem_limit_bytes=V7X_VMEM_LIMIT),
        name="mixer",
    )(x, pos, mod, g, invf, dmask, zeta, xi, cdec, conv_w, w_in, w_ret, w_conv, w_mix)


def _ffn_kernel(x_ref, mod_ref, g_ref, gf_ref, wg_hbm, wu_hbm, wd_hbm, o_ref,
                wg_ref, wu_ref, wd_ref, stage_ref, sem_ref):
    @pl.when(_is_first_step())
    def _():
        _stream_cast(_col_chunks(wg_hbm, wg_ref, FFN_STAGE_COLS) + _col_chunks(wu_hbm, wu_ref, FFN_STAGE_COLS)
                     + _row_chunks(wd_hbm, wd_ref, stage_ref.shape[1]), stage_ref, sem_ref)

    shift, scale, gate = mod_ref[3:4, :], mod_ref[4:5, :], mod_ref[5:6, :]
    n_sub = x_ref.shape[0] // FFN_SUB_ROWS

    def hidden(i):
        x = x_ref[pl.ds(i * FFN_SUB_ROWS, FFN_SUB_ROWS), :]
        h = (_rmsnorm(x, g_ref[...]) * (1.0 + scale) + shift).astype(jnp.bfloat16)
        a = _dot(h, wg_ref[...])
        return (a * _sigmoid(a) * _dot(h, wu_ref[...])).astype(jnp.bfloat16)

    def finish(i, a):
        rows = pl.ds(i * FFN_SUB_ROWS, FFN_SUB_ROWS)
        x2 = x_ref[rows, :] + gate * _dot(a, wd_ref[...])
        o_ref[rows, :] = _rmsnorm(x2, gf_ref[...])

    a_prev = hidden(0)
    for i in range(1, n_sub):
        a_next = hidden(i)
        finish(i - 1, a_prev)
        a_prev = a_next
    finish(n_sub - 1, a_prev)


def _ffn(x, mod, g, gf, wg, wu, wd):
    B, S, D = x.shape
    TM = FFN_ROWS
    return pl.pallas_call(
        _ffn_kernel,
        out_shape=jax.ShapeDtypeStruct((B, S, D), jnp.float32),
        grid=(B, S // TM),
        in_specs=[
            pl.BlockSpec((None, TM, D), lambda b, s: (b, s, 0)),
            pl.BlockSpec((None, N_MOD, D), lambda b, s: (b, 0, 0)),
            _resident((1, D)), _resident((1, D)), _hbm(), _hbm(), _hbm(),
        ],
        out_specs=pl.BlockSpec((None, TM, D), lambda b, s: (b, s, 0)),
        scratch_shapes=[
            pltpu.VMEM(wg.shape, jnp.bfloat16), pltpu.VMEM(wu.shape, jnp.bfloat16), pltpu.VMEM(wd.shape, jnp.bfloat16),
            pltpu.VMEM((CAST_BUFFERS, D, FFN_STAGE_COLS), jnp.float32), pltpu.SemaphoreType.DMA((CAST_BUFFERS,)),
        ],
        compiler_params=pltpu.CompilerParams(
            dimension_semantics=("arbitrary", "arbitrary"), vmem_limit_bytes=V7X_VMEM_LIMIT),
        name="ffn",
    )(x, mod, g, gf, wg, wu, wd)


def _retention_tables(ts):
    H, d, C = N_HEADS, HEAD_DIM, CHUNK
    log_gamma = np.log(1.0 - 2.0 ** (-5.0 - np.arange(H, dtype=np.float64)))
    idx = np.arange(C, dtype=np.float64)
    rel = idx[:, None] - idx[None, :]
    dmask = np.where(rel >= 0, np.exp(log_gamma[:, None, None] * np.maximum(rel, 0.0)), 0.0)
    zeta = np.exp(log_gamma[:, None] * (C - 1 - idx)[None, :])
    xi = np.exp(log_gamma[:, None] * (idx + 1.0)[None, :])
    cdec = np.exp(log_gamma * C)
    inv_freq = 1.0 / (ROPE_BASE ** (np.arange(0, d, 2, dtype=np.float64) / d))
    bc = lambda t: np.broadcast_to(t[:, :, None], (H, C, d))
    tables = (np.broadcast_to(inv_freq[:, None], (d // 2, ts)), dmask, bc(zeta), bc(xi),
              np.broadcast_to(cdec[:, None, None], (H, 1, d)))
    return tuple(jnp.asarray(np.ascontiguousarray(t), dtype=jnp.float32) for t in tables)


def kernel(x, c, positions, ada_w, ada_b, norm_mix_g, w_in, conv_w, ret_w_out, conv_w_out, mix_w_out,
           norm_ffn_g, ffn_w_gate, ffn_w_up, ffn_w_down, final_norm_g):
    B, S, D = x.shape
    depth = ada_w.shape[0]
    assert depth == 1, "the final RMSNorm is fused into the (single) layer's ffn call"
    tables = _retention_tables(MIX_ROWS)
    pos = positions.reshape(B, 1, S)
    mod = _adaln_mod(c, ada_w[0], ada_b[0]).reshape(B, N_MOD, D)
    x = _mixer(x, pos, mod, norm_mix_g[0].reshape(1, D), tables, w_in[0], conv_w[0],
               ret_w_out[0], conv_w_out[0], mix_w_out[0])
    return _ffn(x, mod, norm_ffn_g[0].reshape(1, D), final_norm_g.reshape(1, D),
                ffn_w_gate[0], ffn_w_up[0], ffn_w_down[0])
```

```python
import functools

import jax
import jax.numpy as jnp
import numpy as np
from jax.experimental import pallas as pl
from jax.experimental.pallas import tpu as pltpu

N_HEADS = 8
HEAD_DIM = 128
CHUNK = 128
CONV_K = 3
N_MOD = 6
EPS = 1e-6
ROPE_BASE = 10000.0
CONV_HALO = 8

MIX_ROWS = 512
MIX_SUB_ROWS = 256
FFN_ROWS = 1024
FFN_SUB_ROWS = 256
MOD_COLS = 1536
CAST_ROWS = 512
FFN_CAST_COLS = 1408
CAST_BUFFERS = 4
V7X_VMEM_LIMIT = 56 * 1024 * 1024


def _resident(shape):
    nd = len(shape)
    return pl.BlockSpec(shape, lambda *_: (0,) * nd, pipeline_mode=pl.Buffered(1))


def _sigmoid(t):
    return 1.0 / (1.0 + jnp.exp(-t))


def _rmsnorm(x, g):
    return x * jax.lax.rsqrt(jnp.mean(x * x, axis=-1, keepdims=True) + EPS) * g


def _dot(a, b):
    return jnp.dot(a, b, preferred_element_type=jnp.float32)


def _hbm():
    return pl.BlockSpec(memory_space=pl.ANY)


def _is_first_step():
    return (pl.program_id(0) == 0) & (pl.program_id(1) == 0)


def _stream_cast(chunks, stage_ref, sem_ref):
    n_buf = stage_ref.shape[0]

    def copy(i):
        rows, cols = chunks[i][0].shape
        slot = i % n_buf
        return pltpu.make_async_copy(chunks[i][0], stage_ref.at[slot, pl.ds(0, rows), pl.ds(0, cols)],
                                     sem_ref.at[slot])

    for i in range(min(n_buf, len(chunks))):
        copy(i).start(priority=i % 2)
    for i, (src, dst) in enumerate(chunks):
        rows, cols = src.shape
        copy(i).wait()
        dst[...] = stage_ref[i % n_buf, pl.ds(0, rows), pl.ds(0, cols)].astype(jnp.bfloat16)
        if i + n_buf < len(chunks):
            copy(i + n_buf).start(priority=(i + n_buf) % 2)


def _tile_chunks(src, dst, max_rows, max_cols):
    n_rows, n_cols = src.shape
    chunks = []
    for r in range(0, n_rows, max_rows):
        for c in range(0, n_cols, max_cols):
            view = (pl.ds(r, min(max_rows, n_rows - r)), pl.ds(c, min(max_cols, n_cols - c)))
            chunks.append((src.at[view], dst.at[view]))
    return chunks


def _adaln_kernel(c_ref, w_hbm, b_ref, o_ref, w_ref, sem_ref):
    n_chunks = w_ref.shape[1] // MOD_COLS
    copies = [pltpu.make_async_copy(w_hbm.at[:, pl.ds(j * MOD_COLS, MOD_COLS)],
                                    w_ref.at[:, pl.ds(j * MOD_COLS, MOD_COLS)], sem_ref.at[j])
              for j in range(n_chunks)]
    for j, cp in enumerate(copies):
        cp.start(priority=j % 2)
    c = c_ref[...]
    cs = (c * _sigmoid(c)).astype(jnp.bfloat16)
    for j, cp in enumerate(copies):
        cols = pl.ds(j * MOD_COLS, MOD_COLS)
        cp.wait()
        o_ref[:, cols] = _dot(cs, w_ref[:, cols].astype(jnp.bfloat16)) + b_ref[:, cols]


def _adaln_mod(c, ada_w, ada_b):
    B, D = c.shape
    N = ada_w.shape[1]
    vmem = pl.BlockSpec(memory_space=pltpu.VMEM)
    return pl.pallas_call(
        _adaln_kernel,
        out_shape=jax.ShapeDtypeStruct((B, N), jnp.float32),
        in_specs=[vmem, _hbm(), vmem],
        out_specs=vmem,
        scratch_shapes=[pltpu.VMEM((D, N), jnp.float32), pltpu.SemaphoreType.DMA((N // MOD_COLS,))],
        compiler_params=pltpu.CompilerParams(vmem_limit_bytes=V7X_VMEM_LIMIT),
        name="adaln_mod",
    )(c, ada_w, ada_b.reshape(1, N))


def _mixer_kernel(x_ref, pos_ref, mod_ref, g_ref, invf_ref,
                  dmask_ref, zeta_ref, xi_ref, cdec_ref,
                  conv_w_ref, w_in_hbm, w_ret_hbm, w_conv_hbm, w_mix_hbm,
                  o_ref,
                  state_ref, ubuf_ref, yret_ref,
                  w_in_ref, w_ret_ref, w_conv_ref, w_mix_ref, stage_ref, sem_ref):
    D = x_ref.shape[-1]
    TS = x_ref.shape[0]
    SUB = MIX_SUB_ROWS
    W = N_HEADS * HEAD_DIM
    PAIR = 2 * HEAD_DIM
    n_pairs = N_HEADS // 2

    @pl.when(_is_first_step())
    def _():
        tile = stage_ref.shape[1:]
        _stream_cast(_tile_chunks(w_in_hbm, w_in_ref, *tile) + _tile_chunks(w_ret_hbm, w_ret_ref, *tile)
                     + _tile_chunks(w_conv_hbm, w_conv_ref, *tile) + _tile_chunks(w_mix_hbm, w_mix_ref, *tile),
                     stage_ref, sem_ref)

    @pl.when(pl.program_id(1) == 0)
    def _():
        state_ref[...] = jnp.zeros_like(state_ref)
        ubuf_ref[0:CONV_HALO, :] = jnp.zeros((CONV_HALO, D), jnp.float32)

    shift, scale, gate = mod_ref[0:1, :], mod_ref[1:2, :], mod_ref[2:3, :]

    ang_t = invf_ref[...] * pos_ref[...].astype(jnp.float32)
    cos_h, sin_h = jnp.cos(ang_t), jnp.sin(ang_t)
    cos = jnp.concatenate([cos_h, cos_h], axis=0).T
    sin = jnp.concatenate([-sin_h, sin_h], axis=0).T
    k_scale = HEAD_DIM ** -0.5
    cos_k, sin_k = cos * k_scale, sin * k_scale

    def rope(t, c, s):
        return t * c + pltpu.roll(t, HEAD_DIM // 2, 1) * s

    def sub_tile(row0):
        x = x_ref[pl.ds(row0, SUB), :]
        h = (_rmsnorm(x, g_ref[...]) * (1.0 + scale) + shift).astype(jnp.bfloat16)

        def proj(col, width):
            return _dot(h, w_in_ref[:, pl.ds(col, width)])

        def project_pair(hp):
            return [proj(sec * W + hp * PAIR, PAIR) for sec in range(4)]

        def score_stage(hp, qkvg):
            q2, k2, v2, g2 = qkvg
            units = []
            for j in range(2):
                hd = 2 * hp + j
                lanes = slice(j * HEAD_DIM, (j + 1) * HEAD_DIM)
                for ck in range(SUB // CHUNK):
                    rows = slice(ck * CHUNK, (ck + 1) * CHUNK)
                    trows = slice(row0 + ck * CHUNK, row0 + (ck + 1) * CHUNK)
                    q = rope(q2[rows, lanes], cos[trows], sin[trows])
                    k = rope(k2[rows, lanes], cos_k[trows], sin_k[trows])
                    v = v2[rows, lanes]
                    qb, kb = q.astype(jnp.bfloat16), k.astype(jnp.bfloat16)
                    scores = jax.lax.dot_general(qb, kb, (((1,), (1,)), ((), ())),
                                                 preferred_element_type=jnp.float32) * dmask_ref[hd]
                    kv = jax.lax.dot_general(kb, (v * zeta_ref[hd]).astype(jnp.bfloat16),
                                             (((0,), (0,)), ((), ())),
                                             preferred_element_type=jnp.float32)
                    lhs = jnp.concatenate([scores.astype(jnp.bfloat16),
                                           (q * xi_ref[hd]).astype(jnp.bfloat16)], axis=1)
                    units.append((hd, row0 + ck * CHUNK, lhs, v.astype(jnp.bfloat16), kv, g2[rows, lanes]))
            return units

        def output_stage(units):
            for hd, trow, lhs, vb, kv, g_ret in units:
                r_prev = state_ref[hd]
                out = _dot(lhs, jnp.concatenate([vb, r_prev.astype(jnp.bfloat16)], axis=0))
                state_ref[hd] = r_prev * cdec_ref[hd] + kv
                mu = jnp.mean(out, axis=-1, keepdims=True)
                cen = out - mu
                var = jnp.mean(cen * cen, axis=-1, keepdims=True)
                y = cen * jax.lax.rsqrt(var + EPS)
                yret_ref[pl.ds(trow, CHUNK), pl.ds(hd * HEAD_DIM, HEAD_DIM)] = (
                    g_ret * _sigmoid(g_ret) * y).astype(jnp.bfloat16)

        fillers = [lambda: proj(5 * W, D), lambda: proj(6 * W, D), lambda: proj(4 * W, D)]
        filled = []
        pending = [project_pair(0), project_pair(1)]
        for hp in range(n_pairs):
            units = score_stage(hp, pending.pop(0))
            if hp + 2 < n_pairs:
                pending.append(project_pair(hp + 2))
            else:
                filled.append(fillers.pop(0)())
            output_stage(units)
        filled.append(fillers.pop(0)())
        c_cv, u_cv, b_cv = filled

        u = c_cv * u_cv
        base = CONV_HALO + row0
        ubuf_ref[pl.ds(base, SUB), :] = u
        conv = u * conv_w_ref[CONV_K - 1:CONV_K, :]
        for tap in range(CONV_K - 1):
            back = CONV_K - 1 - tap
            conv = conv + ubuf_ref[pl.ds(base - back, SUB), :] * conv_w_ref[tap:tap + 1, :]
        g_a = proj(7 * W, D)
        y_a = _dot(yret_ref[pl.ds(row0, SUB), :], w_ret_ref[...])
        g_b = proj(7 * W + D, D)
        y_b = _dot((b_cv * conv).astype(jnp.bfloat16), w_conv_ref[...])

        merged = _sigmoid(g_a) * y_a + _sigmoid(g_b) * y_b
        o_ref[pl.ds(row0, SUB), :] = x + gate * _dot(merged.astype(jnp.bfloat16), w_mix_ref[...])

    for sub in range(TS // SUB):
        sub_tile(sub * SUB)
    ubuf_ref[0:CONV_HALO, :] = ubuf_ref[TS:TS + CONV_HALO, :]


def _mixer(x, pos, mod, g, tables, w_in, conv_w, w_ret, w_conv, w_mix):
    B, S, D = x.shape
    TS = MIX_ROWS
    invf, dmask, zeta, xi, cdec = tables
    n_in = w_in.shape[1]
    return pl.pallas_call(
        _mixer_kernel,
        out_shape=jax.ShapeDtypeStruct((B, S, D), jnp.float32),
        grid=(B, S // TS),
        in_specs=[
            pl.BlockSpec((None, TS, D), lambda b, s: (b, s, 0)),
            pl.BlockSpec((None, 1, TS), lambda b, s: (b, 0, s)),
            pl.BlockSpec((None, N_MOD, D), lambda b, s: (b, 0, 0)),
            _resident((1, D)),
            _resident(invf.shape),
            _resident(dmask.shape), _resident(zeta.shape), _resident(xi.shape), _resident(cdec.shape),
            _resident(conv_w.shape), _hbm(), _hbm(), _hbm(), _hbm(),
        ],
        out_specs=pl.BlockSpec((None, TS, D), lambda b, s: (b, s, 0)),
        scratch_shapes=[
            pltpu.VMEM((N_HEADS, HEAD_DIM, HEAD_DIM), jnp.float32),
            pltpu.VMEM((TS + CONV_HALO, D), jnp.float32),
            pltpu.VMEM((TS, N_HEADS * HEAD_DIM), jnp.bfloat16),
            pltpu.VMEM((D, n_in), jnp.bfloat16),
            pltpu.VMEM((D, D), jnp.bfloat16), pltpu.VMEM((D, D), jnp.bfloat16), pltpu.VMEM((D, D), jnp.bfloat16),
            pltpu.VMEM((CAST_BUFFERS, CAST_ROWS, D), jnp.float32), pltpu.SemaphoreType.DMA((CAST_BUFFERS,)),
        ],
        compiler_params=pltpu.CompilerParams(
            dimension_semantics=("arbitrary", "arbitrary"), vmem_limit_bytes=V7X_VMEM_LIMIT),
        name="mixer",
    )(x, pos, mod, g, invf, dmask, zeta, xi, cdec, conv_w, w_in, w_ret, w_conv, w_mix)


def _ffn_kernel(x_ref, mod_ref, g_ref, gf_ref, wg_hbm, wu_hbm, wd_hbm, o_ref,
                wg_ref, wu_ref, wd_ref, stage_ref, sem_ref):
    @pl.when(_is_first_step())
    def _():
        tile = stage_ref.shape[1:]
        _stream_cast(_tile_chunks(wg_hbm, wg_ref, *tile) + _tile_chunks(wu_hbm, wu_ref, *tile)
                     + _tile_chunks(wd_hbm, wd_ref, *tile), stage_ref, sem_ref)

    shift, scale, gate = mod_ref[3:4, :], mod_ref[4:5, :], mod_ref[5:6, :]
    n_sub = x_ref.shape[0] // FFN_SUB_ROWS

    def hidden(i):
        x = x_ref[pl.ds(i * FFN_SUB_ROWS, FFN_SUB_ROWS), :]
        h = (_rmsnorm(x, g_ref[...]) * (1.0 + scale) + shift).astype(jnp.bfloat16)
        a = _dot(h, wg_ref[...])
        return (a * _sigmoid(a) * _dot(h, wu_ref[...])).astype(jnp.bfloat16)

    def finish(i, a):
        rows = pl.ds(i * FFN_SUB_ROWS, FFN_SUB_ROWS)
        x2 = x_ref[rows, :] + gate * _dot(a, wd_ref[...])
        o_ref[rows, :] = _rmsnorm(x2, gf_ref[...])

    a_prev = hidden(0)
    for i in range(1, n_sub):
        a_next = hidden(i)
        finish(i - 1, a_prev)
        a_prev = a_next
    finish(n_sub - 1, a_prev)


def _ffn(x, mod, g, gf, wg, wu, wd):
    B, S, D = x.shape
    TM = FFN_ROWS
    return pl.pallas_call(
        _ffn_kernel,
        out_shape=jax.ShapeDtypeStruct((B, S, D), jnp.float32),
        grid=(B, S // TM),
        in_specs=[
            pl.BlockSpec((None, TM, D), lambda b, s: (b, s, 0)),
            pl.BlockSpec((None, N_MOD, D), lambda b, s: (b, 0, 0)),
            _resident((1, D)), _resident((1, D)), _hbm(), _hbm(), _hbm(),
        ],
        out_specs=pl.BlockSpec((None, TM, D), lambda b, s: (b, s, 0)),
        scratch_shapes=[
            pltpu.VMEM(wg.shape, jnp.bfloat16), pltpu.VMEM(wu.shape, jnp.bfloat16), pltpu.VMEM(wd.shape, jnp.bfloat16),
            pltpu.VMEM((CAST_BUFFERS, CAST_ROWS, FFN_CAST_COLS), jnp.float32), pltpu.SemaphoreType.DMA((CAST_BUFFERS,)),
        ],
        compiler_params=pltpu.CompilerParams(
            dimension_semantics=("arbitrary", "arbitrary"), vmem_limit_bytes=V7X_VMEM_LIMIT),
        name="ffn",
    )(x, mod, g, gf, wg, wu, wd)


def _retention_tables(ts):
    H, d, C = N_HEADS, HEAD_DIM, CHUNK
    log_gamma = np.log(1.0 - 2.0 ** (-5.0 - np.arange(H, dtype=np.float64)))
    idx = np.arange(C, dtype=np.float64)
    rel = idx[:, None] - idx[None, :]
    dmask = np.where(rel >= 0, np.exp(log_gamma[:, None, None] * np.maximum(rel, 0.0)), 0.0)
    zeta = np.exp(log_gamma[:, None] * (C - 1 - idx)[None, :])
    xi = np.exp(log_gamma[:, None] * (idx + 1.0)[None, :])
    cdec = np.exp(log_gamma * C)
    inv_freq = 1.0 / (ROPE_BASE ** (np.arange(0, d, 2, dtype=np.float64) / d))
    bc = lambda t: np.broadcast_to(t[:, :, None], (H, C, d))
    tables = (np.broadcast_to(inv_freq[:, None], (d // 2, ts)), dmask, bc(zeta), bc(xi),
              np.broadcast_to(cdec[:, None, None], (H, 1, d)))
    return tuple(jnp.asarray(np.ascontiguousarray(t), dtype=jnp.float32) for t in tables)


def kernel(x, c, positions, ada_w, ada_b, norm_mix_g, w_in, conv_w, ret_w_out, conv_w_out, mix_w_out,
           norm_ffn_g, ffn_w_gate, ffn_w_up, ffn_w_down, final_norm_g):
    B, S, D = x.shape
    depth = ada_w.shape[0]
    assert depth == 1, "the final RMSNorm is fused into the (single) layer's ffn call"
    tables = _retention_tables(MIX_ROWS)
    pos = positions.reshape(B, 1, S)
    mod = _adaln_mod(c, ada_w[0], ada_b[0]).reshape(B, N_MOD, D)
    x = _mixer(x, pos, mod, norm_mix_g[0].reshape(1, D), tables, w_in[0], conv_w[0],
               ret_w_out[0], conv_w_out[0], mix_w_out[0])
    return _ffn(x, mod, norm_ffn_g[0].reshape(1, D), final_norm_g.reshape(1, D),
                ffn_w_gate[0], ffn_w_up[0], ffn_w_down[0])
```
